```python
import math
import jax, jax.numpy as jnp
from jax import lax
import numpy as np

D_MODEL = 1024
BATCH = 2
SEQ = 8192
DEPTH = 2
DEC_BATCH = 128
DEC_SEQ = 8
PAST_LEN = 2048
PAGE_SIZE = 128

HEAD_DIM = 64
H_A = 8
W_A = H_A * HEAD_DIM
MOBA_BLOCK = 256
MOBA_TOPK = 3
MOBA_QC = 64
C_B = 512
POOL_GROUPS = 4
POOL_GC = C_B // POOL_GROUPS
POOL_WINDOWS = (2, 4, 8, 16)
POOL_STATE = 15
C_C = 512
CONV_W = 31
H_D = 4
W_D = H_D * 2 * HEAD_DIM
DIFF_QB = 128
N_BRANCH = 4
BRANCH_W = 512
D_FF = 2816
FFN_CONV_W = 3
EPS = 1e-6
NEG = -1e30

kernel_name = 'moba_pool_conformer_diffattn_gated_hybrid'


def rmsnorm(x, g):
    xf = x.astype(jnp.float32)
    y = xf * lax.rsqrt(jnp.mean(xf * xf, axis=-1, keepdims=True) + EPS)
    return (y * g.astype(jnp.float32)).astype(x.dtype)


def layernorm(x, g, b):
    xf = x.astype(jnp.float32)
    mu = jnp.mean(xf, axis=-1, keepdims=True)
    var = jnp.mean(jnp.square(xf - mu), axis=-1, keepdims=True)
    y = (xf - mu) * lax.rsqrt(var + EPS)
    return (y * g.astype(jnp.float32) + b.astype(jnp.float32)).astype(x.dtype)


def alibi_slopes(n_heads):
    return 2.0 ** (-8.0 * jnp.arange(1, n_heads + 1, dtype=jnp.float32) / n_heads)


def causal_dwconv(ext, w, b):
    c = ext.shape[-1]
    y = lax.conv_general_dilated(ext, w[:, None, :].astype(ext.dtype), window_strides=(1,), padding='VALID',
                                 dimension_numbers=('NWC', 'WIO', 'NWC'), feature_group_count=c)
    return y + b.astype(ext.dtype)


def moba_attention(q, k_all, v_all, pos0, slopes):
    B, T, H, dh = q.shape
    L = k_all.shape[1]
    nb = max(-(-L // MOBA_BLOCK), MOBA_TOPK)
    pad = nb * MOBA_BLOCK - L
    if pad > 0:
        k_all = jnp.pad(k_all, ((0, 0), (0, pad), (0, 0), (0, 0)))
        v_all = jnp.pad(v_all, ((0, 0), (0, pad), (0, 0), (0, 0)))
    kb = k_all.reshape(B, nb, MOBA_BLOCK, H, dh)
    vb = v_all.reshape(B, nb, MOBA_BLOCK, H, dh)
    k_mean = jnp.mean(kb.astype(jnp.float32), axis=2)
    qc = MOBA_QC if T % MOBA_QC == 0 else T
    nc = T // qc
    q_c = q.reshape(B, nc, qc, H, dh).transpose(1, 0, 2, 3, 4)
    qpos_c = (pos0 + jnp.arange(T, dtype=jnp.int32)).reshape(nc, qc)
    b_idx = jnp.arange(B)[:, None, None, None]
    h_idx = jnp.arange(H)[None, None, :, None]
    blk_ids = jnp.arange(nb, dtype=jnp.int32)
    offs = jnp.arange(MOBA_BLOCK, dtype=jnp.int32)
    scale = dh ** -0.5

    def chunk(args):
        qq, qp = args
        own = qp // MOBA_BLOCK
        gate = jnp.einsum('bthd,bjhd->bthj', qq.astype(jnp.float32), k_mean)
        fully_past = blk_ids[None, :] < own[:, None]
        gate = jnp.where(fully_past[None, :, None, :], gate, NEG)
        _, top = lax.top_k(gate, MOBA_TOPK)
        own_b = jnp.broadcast_to(own[None, :, None, None], (B, qc, H, 1))
        sel = jnp.concatenate([top.astype(jnp.int32), own_b], axis=-1)
        valid = jnp.concatenate([top < own_b, jnp.ones((B, qc, H, 1), bool)], axis=-1)
        kg = kb[b_idx, sel, :, h_idx]
        vg = vb[b_idx, sel, :, h_idx]
        s = jnp.einsum('bthd,bthjsd->bthjs', qq, kg).astype(jnp.float32) * scale
        kpos = sel[..., None] * MOBA_BLOCK + offs
        qpb = qp[None, :, None, None, None]
        s = s - slopes[None, None, :, None, None] * (qpb - kpos).astype(jnp.float32)
        s = jnp.where(valid[..., None] & (kpos <= qpb), s, NEG)
        p = jax.nn.softmax(s.reshape(B, qc, H, -1), axis=-1).reshape(s.shape)
        return jnp.einsum('bthjs,bthjsd->bthd', p.astype(vg.dtype), vg)

    out = lax.map(chunk, (q_c, qpos_c))
    return out.transpose(1, 0, 2, 3, 4).reshape(B, T, H * dh)


def diff_attention(q, k_all, v_all, pos0, slopes, lam, lam_init, head_gain):
    B, T, H, _, dh = q.shape
    L = k_all.shape[1]
    qb = DIFF_QB if T % DIFF_QB == 0 else T
    nq = T // qb
    q_c = q.reshape(B, nq, qb, H, 2, dh).transpose(1, 0, 2, 3, 4, 5)
    qpos_c = (pos0 + jnp.arange(T, dtype=jnp.int32)).reshape(nq, qb)
    kpos = jnp.arange(L, dtype=jnp.int32)
    scale = dh ** -0.5

    def block(args):
        qq, qp = args
        s = jnp.einsum('bthcd,bshcd->bhcts', qq, k_all).astype(jnp.float32) * scale
        dist = (qp[:, None] - kpos[None, :]).astype(jnp.float32)
        s = s - slopes[None, :, None, None, None] * dist
        s = jnp.where((kpos[None, :] <= qp[:, None])[None, None, None], s, NEG)
        p = jax.nn.softmax(s, axis=-1)
        a = p[:, :, 0] - lam * p[:, :, 1]
        return jnp.einsum('bhts,bshe->bthe', a.astype(v_all.dtype), v_all)

    o = lax.map(block, (q_c, qpos_c))
    o = o.transpose(1, 0, 2, 3, 4).reshape(B, T, H, 2 * dh)
    o = rmsnorm(o, head_gain) * (1.0 - lam_init)
    return o.reshape(B, T, H * 2 * dh)


def pool_mixer(u, state, pos0, w_grp, ls):
    B, T, C = u.shape
    ext = jnp.concatenate([state, u], axis=1)
    cs = jnp.cumsum(ext.astype(jnp.float32), axis=1)
    cs = jnp.pad(cs, ((0, 0), (1, 0), (0, 0)))
    pos = pos0 + jnp.arange(T, dtype=jnp.int32)
    parts = []
    for g, w in enumerate(POOL_WINDOWS):
        lo, hi = g * POOL_GC, (g + 1) * POOL_GC
        total = cs[:, POOL_STATE + 1:POOL_STATE + 1 + T, lo:hi] - cs[:, POOL_STATE + 1 - w:POOL_STATE + 1 - w + T, lo:hi]
        cnt = jnp.minimum(w, pos + 1).astype(jnp.float32)[None, :, None]
        parts.append(total / cnt)
    mean = jnp.concatenate(parts, axis=-1).astype(u.dtype)
    d = (mean - u).reshape(B, T, POOL_GROUPS, POOL_GC)
    y = jnp.einsum('btgc,gce->btge', d, w_grp).reshape(B, T, C)
    return y * ls, ext[:, -POOL_STATE:]


def conformer_conv(uc, state, w_dw, b_dw, ln_g, ln_b):
    a, b = jnp.split(uc, 2, axis=-1)
    u = a * jax.nn.sigmoid(b)
    ext = jnp.concatenate([state, u], axis=1)
    h = causal_dwconv(ext, w_dw, b_dw)
    h = jax.nn.silu(layernorm(h, ln_g, ln_b))
    return h, ext[:, -(CONV_W - 1):]


def gather_pages(cache_l, page_table):
    g = cache_l[page_table]
    return g.reshape(page_table.shape[0], page_table.shape[1] * cache_l.shape[1], *cache_l.shape[2:])


def layer_forward(x, pos0, past_ak, past_av, past_dk, past_dv, st_pool, st_conv, st_ffn, lp, lam_init):
    B, T, _ = x.shape
    h = rmsnorm(x, lp['norm_mix'])
    proj = h @ lp['w_in']
    sizes = (W_A, W_A, W_A, C_B, 2 * C_C, W_D, W_D, W_D, N_BRANCH * D_MODEL)
    cuts = [int(c) for c in np.cumsum(sizes)[:-1]]
    qa, ka, va, ub, uc, qd, kd, vd, gates = jnp.split(proj, cuts, axis=-1)
    qa = rmsnorm(qa.reshape(B, T, H_A, HEAD_DIM), lp['qnorm_a'])
    ka = rmsnorm(ka.reshape(B, T, H_A, HEAD_DIM), lp['knorm_a'])
    va = va.reshape(B, T, H_A, HEAD_DIM)
    ka_all = ka if past_ak is None else jnp.concatenate([past_ak, ka], axis=1)
    va_all = va if past_av is None else jnp.concatenate([past_av, va], axis=1)
    ya = moba_attention(qa, ka_all, va_all, pos0, alibi_slopes(H_A))
    yb, new_pool = pool_mixer(ub, st_pool, pos0, lp['w_pool'], lp['ls_pool'])
    yc, new_conv = conformer_conv(uc, st_conv, lp['dw_conv'], lp['b_conv'], lp['ln_conv_g'], lp['ln_conv_b'])
    qd = rmsnorm(qd.reshape(B, T, H_D, 2, HEAD_DIM), lp['qnorm_d'])
    kd = rmsnorm(kd.reshape(B, T, H_D, 2, HEAD_DIM), lp['knorm_d'])
    vd = vd.reshape(B, T, H_D, 2 * HEAD_DIM)
    kd_all = kd if past_dk is None else jnp.concatenate([past_dk, kd], axis=1)
    vd_all = vd if past_dv is None else jnp.concatenate([past_dv, vd], axis=1)
    lam = (jnp.exp(jnp.sum((lp['lam_q1'] * lp['lam_k1']).astype(jnp.float32)))
           - jnp.exp(jnp.sum((lp['lam_q2'] * lp['lam_k2']).astype(jnp.float32))) + lam_init)
    yd = diff_attention(qd, kd_all, vd_all, pos0, alibi_slopes(H_D), lam, lam_init, lp['hnorm_d'])
    br = jnp.einsum('btnc,ncd->btnd', jnp.stack([ya, yb, yc, yd], axis=2), lp['w_branch'])
    g = jax.nn.sigmoid((gates + lp['b_gate']).astype(jnp.float32)).reshape(B, T, N_BRANCH, D_MODEL)
    merged = jnp.sum(g.astype(br.dtype) * br, axis=2)
    x = x + merged @ lp['w_out']
    h2 = rmsnorm(x, lp['norm_ffn'])
    gf, vf = jnp.split(h2 @ lp['w_up'], 2, axis=-1)
    ext = jnp.concatenate([st_ffn, gf], axis=1)
    gc = causal_dwconv(ext, lp['w_ffn_conv'], lp['b_ffn_conv'])
    x = x + (jax.nn.gelu(gc) * vf) @ lp['w_down']
    return x, (ka, va, kd, vd, new_pool, new_conv, ext[:, -(FFN_CONV_W - 1):])


def setup_inputs(seed: int = 0) -> dict:
    key = jax.random.key(seed)
    ks = list(jax.random.split(key, 40))
    it = iter(ks)

    def nrm(shape, scale=1.0):
        return scale * jax.random.normal(next(it), shape, jnp.float32)

    def gain(shape):
        return 1.0 + 0.05 * jax.random.normal(next(it), shape, jnp.float32)

    n_pages = PAST_LEN // PAGE_SIZE
    n_used = DEC_BATCH * n_pages
    n_pool = (n_used * 5) // 4
    perm = jax.random.permutation(next(it), n_pool)[:n_used]
    page_table = perm.reshape(DEC_BATCH, n_pages).astype(jnp.int32)
    d_in = 3 * W_A + C_B + 2 * C_C + 3 * W_D + N_BRANCH * D_MODEL
    return {
        'x_prompt': nrm((BATCH, SEQ, D_MODEL)),
        'x_sample': nrm((DEC_BATCH, DEC_SEQ, D_MODEL)),
        'cache_a_k': nrm((DEPTH, n_pool, PAGE_SIZE, H_A, HEAD_DIM)),
        'cache_a_v': nrm((DEPTH, n_pool, PAGE_SIZE, H_A, HEAD_DIM)),
        'cache_d_k': nrm((DEPTH, n_pool, PAGE_SIZE, H_D, 2, HEAD_DIM)),
        'cache_d_v': nrm((DEPTH, n_pool, PAGE_SIZE, H_D, 2 * HEAD_DIM)),
        'state_pool': nrm((DEPTH, DEC_BATCH, POOL_STATE, C_B)),
        'state_conv': nrm((DEPTH, DEC_BATCH, CONV_W - 1, C_C)),
        'state_ffn': nrm((DEPTH, DEC_BATCH, FFN_CONV_W - 1, D_FF)),
        'page_table': page_table,
        'norm_mix': gain((DEPTH, D_MODEL)),
        'w_in': nrm((DEPTH, D_MODEL, d_in), D_MODEL ** -0.5),
        'b_gate': nrm((DEPTH, N_BRANCH * D_MODEL), 0.1),
        'qnorm_a': gain((DEPTH, HEAD_DIM)),
        'knorm_a': gain((DEPTH, HEAD_DIM)),
        'w_pool': nrm((DEPTH, POOL_GROUPS, POOL_GC, POOL_GC), POOL_GC ** -0.5),
        'ls_pool': gain((DEPTH, C_B)),
        'dw_conv': nrm((DEPTH, CONV_W, C_C), CONV_W ** -0.5),
        'b_conv': nrm((DEPTH, C_C), 0.02),
        'ln_conv_g': gain((DEPTH, C_C)),
        'ln_conv_b': nrm((DEPTH, C_C), 0.02),
        'qnorm_d': gain((DEPTH, HEAD_DIM)),
        'knorm_d': gain((DEPTH, HEAD_DIM)),
        'lam_q1': nrm((DEPTH, HEAD_DIM), 0.1),
        'lam_k1': nrm((DEPTH, HEAD_DIM), 0.1),
        'lam_q2': nrm((DEPTH, HEAD_DIM), 0.1),
        'lam_k2': nrm((DEPTH, HEAD_DIM), 0.1),
        'hnorm_d': gain((DEPTH, 2 * HEAD_DIM)),
        'w_branch': nrm((DEPTH, N_BRANCH, BRANCH_W, D_MODEL), BRANCH_W ** -0.5),
        'w_out': nrm((DEPTH, D_MODEL, D_MODEL), D_MODEL ** -0.5),
        'norm_ffn': gain((DEPTH, D_MODEL)),
        'w_up': nrm((DEPTH, D_MODEL, 2 * D_FF), D_MODEL ** -0.5),
        'w_ffn_conv': nrm((DEPTH, FFN_CONV_W, D_FF), FFN_CONV_W ** -0.5),
        'b_ffn_conv': nrm((DEPTH, D_FF), 0.02),
        'w_down': nrm((DEPTH, D_FF, D_MODEL), D_FF ** -0.5),
    }


def reference(x_prompt, x_sample, cache_a_k, cache_a_v, cache_d_k, cache_d_v, state_pool, state_conv, state_ffn,
              page_table, norm_mix, w_in, b_gate, qnorm_a, knorm_a, w_pool, ls_pool, dw_conv, b_conv, ln_conv_g,
              ln_conv_b, qnorm_d, knorm_d, lam_q1, lam_k1, lam_q2, lam_k2, hnorm_d, w_branch, w_out, norm_ffn, w_up,
              w_ffn_conv, b_ffn_conv, w_down):
    past_len = page_table.shape[1] * cache_a_k.shape[2]
    bp = x_prompt.shape[0]
    yp, ys = x_prompt, x_sample
    p_new = [[] for _ in range(7)]
    s_new = [[] for _ in range(7)]
    for l in range(DEPTH):
        lp = {'norm_mix': norm_mix[l], 'w_in': w_in[l], 'b_gate': b_gate[l], 'qnorm_a': qnorm_a[l],
              'knorm_a': knorm_a[l], 'w_pool': w_pool[l], 'ls_pool': ls_pool[l], 'dw_conv': dw_conv[l],
              'b_conv': b_conv[l], 'ln_conv_g': ln_conv_g[l], 'ln_conv_b': ln_conv_b[l], 'qnorm_d': qnorm_d[l],
              'knorm_d': knorm_d[l], 'lam_q1': lam_q1[l], 'lam_k1': lam_k1[l], 'lam_q2': lam_q2[l],
              'lam_k2': lam_k2[l], 'hnorm_d': hnorm_d[l], 'w_branch': w_branch[l], 'w_out': w_out[l],
              'norm_ffn': norm_ffn[l], 'w_up': w_up[l], 'w_ffn_conv': w_ffn_conv[l], 'b_ffn_conv': b_ffn_conv[l],
              'w_down': w_down[l]}
        lam_init = 0.8 - 0.6 * math.exp(-0.3 * l)
        dt = yp.dtype
        yp, st_p = layer_forward(
            yp, 0, None, None, None, None,
            jnp.zeros((bp, POOL_STATE, C_B), dt), jnp.zeros((bp, CONV_W - 1, C_C), dt),
            jnp.zeros((bp, FFN_CONV_W - 1, D_FF), dt), lp, lam_init)
        ys, st_s = layer_forward(
            ys, past_len,
            gather_pages(cache_a_k[l], page_table), gather_pages(cache_a_v[l], page_table),
            gather_pages(cache_d_k[l], page_table), gather_pages(cache_d_v[l], page_table),
            state_pool[l], state_conv[l], state_ffn[l], lp, lam_init)
        for i in range(7):
            p_new[i].append(st_p[i])
            s_new[i].append(st_s[i])
    p_a_k, p_a_v, p_d_k, p_d_v, p_pool, p_conv, p_ffn = [jnp.stack(a, axis=0) for a in p_new]
    s_a_k, s_a_v, s_d_k, s_d_v, s_pool, s_conv, s_ffn = [jnp.stack(a, axis=0) for a in s_new]
    return (yp, ys, p_a_k, p_a_v, p_d_k, p_d_v, p_pool, p_conv, p_ffn,
            s_a_k, s_a_v, s_d_k, s_d_v, s_pool, s_conv, s_ffn)
```

```python
import functools
import math

import numpy as np
import jax
import jax.numpy as jnp
from jax import lax
from jax.experimental import pallas as pl
from jax.experimental.pallas import tpu as pltpu

F32 = jnp.float32
BF16 = jnp.bfloat16

D_MODEL = 1024
HEAD_DIM = 64
H_A = 8
H_D = 4
MOBA_BLOCK = 256
MOBA_TOPK = 3
C_B = 512
POOL_WINDOWS = (2, 4, 8, 16)
POOL_GC = 128
POOL_STATE = 15
C_C = 512
CONV_W = 31
N_BRANCH = 4
BRANCH_W = 512
D_FF = 2816
FFN_CONV_W = 3
EPS = 1e-6
NEG = -1e30

COL = 512
D_IN = 17 * COL
CT_QA, CT_KA, CT_VA, CT_UB, CT_UC, CT_QD, CT_KD, CT_VD, CT_GATE = 0, 1, 2, 3, 4, 6, 7, 8, 9
AUG = 128
POOL_PAD = 16
CONV_PAD = 32
FFN_PAD = 8
FFN_CK = 256
VMEM_LIMIT = 56 * 1024 * 1024


def _cparams(*sem):
    return pltpu.CompilerParams(dimension_semantics=sem, vmem_limit_bytes=VMEM_LIMIT)


def _alibi_slopes(n_heads):
    return [2.0 ** (-8.0 * i / n_heads) for i in range(1, n_heads + 1)]


def _in_proj_kernel(x_ref, nm_ref, w_ref, gain_ref, bias_ref, gmat_ref, o_ref, h_scr):
    j = pl.program_id(1)

    @pl.when(j == 0)
    def _():
        x = x_ref[...]
        ms = jnp.mean(x * x, axis=-1, keepdims=True)
        h_scr[...] = (x * lax.rsqrt(ms + EPS) * nm_ref[...]).astype(BF16)

    y = jnp.dot(h_scr[...], w_ref[...], preferred_element_type=F32)
    is_norm = (j == CT_QA) | (j == CT_KA) | (j == CT_QD) | (j == CT_KD)
    is_gate = j >= CT_GATE

    @pl.when(is_norm)
    def _():
        ms = jnp.dot((y * y).astype(BF16), gmat_ref[...], preferred_element_type=F32)
        o_ref[...] = y * lax.rsqrt(ms + EPS) * gain_ref[...]

    @pl.when(is_gate)
    def _():
        o_ref[...] = jax.nn.sigmoid(y + bias_ref[...])

    @pl.when(jnp.logical_not(is_norm | is_gate))
    def _():
        o_ref[...] = y


def _in_proj(x, norm_g, w_bf, gain_row, bias_row, gmat):
    n = x.shape[0]
    tm = min(1024, n)
    assert n % tm == 0
    return pl.pallas_call(
        _in_proj_kernel,
        grid=(n // tm, D_IN // COL),
        in_specs=[
            pl.BlockSpec((tm, D_MODEL), lambda i, j: (i, 0)),
            pl.BlockSpec((1, D_MODEL), lambda i, j: (0, 0)),
            pl.BlockSpec((D_MODEL, COL), lambda i, j: (0, j)),
            pl.BlockSpec((1, COL), lambda i, j: (0, j)),
            pl.BlockSpec((1, COL), lambda i, j: (0, j)),
            pl.BlockSpec((COL, COL), lambda i, j: (0, 0)),
        ],
        out_specs=pl.BlockSpec((tm, COL), lambda i, j: (i, j)),
        out_shape=jax.ShapeDtypeStruct((n, D_IN), F32),
        scratch_shapes=[pltpu.VMEM((tm, D_MODEL), BF16)],
        compiler_params=_cparams("arbitrary", "arbitrary"),
        name="in_proj",
    )(x, norm_g, w_bf, gain_row, bias_row, gmat)


def _bc_kernel(ub_ref, ua_ref, ug_ref, sp_ref, sc_ref, wp_ref, ls_ref, dw_ref, bc_ref, lng_ref, lnb_ref,
               yb_ref, yc_ref, u_ref, pext, cext, hscr, *, tm, pos0):
    i = pl.program_id(1)

    @pl.when(i == 0)
    def _():
        pext[0:POOL_PAD, :] = sp_ref[0]
        cext[0:CONV_PAD, :] = sc_ref[0]

    @pl.when(i > 0)
    def _():
        pext[0:POOL_PAD, :] = pext[tm:tm + POOL_PAD, :]
        cext[0:CONV_PAD, :] = cext[tm:tm + CONV_PAD, :]

    ub = ub_ref[...]
    pext[POOL_PAD:POOL_PAD + tm, :] = ub
    u = ua_ref[...] * jax.nn.sigmoid(ug_ref[...])
    cext[CONV_PAD:CONV_PAD + tm, :] = u
    u_ref[...] = u

    pos = pos0 + i * tm + lax.broadcasted_iota(jnp.int32, (tm, 1), 0)
    for g, w in enumerate(POOL_WINDOWS):
        lo = g * POOL_GC
        tot = pext[POOL_PAD:POOL_PAD + tm, lo:lo + POOL_GC]
        for k in range(1, w):
            tot = tot + pext[POOL_PAD - k:POOL_PAD - k + tm, lo:lo + POOL_GC]
        cnt = jnp.minimum(w, pos + 1).astype(F32)
        d = tot / cnt - ub[:, lo:lo + POOL_GC]
        y = jnp.dot(d.astype(BF16), wp_ref[g], preferred_element_type=F32)
        yb_ref[:, lo:lo + POOL_GC] = y * ls_ref[:, lo:lo + POOL_GC]

    rc = min(64, tm)
    first = CONV_PAD - (CONV_W - 1)
    for r0 in range(0, tm, rc):
        for c in range(C_C // 128):
            lo = c * 128
            acc = jnp.zeros((rc, 128), F32)
            for k in range(CONV_W):
                acc = acc + cext[r0 + first + k:r0 + first + k + rc, lo:lo + 128] * dw_ref[k:k + 1, lo:lo + 128]
            hscr[r0:r0 + rc, lo:lo + 128] = acc + bc_ref[:, lo:lo + 128]
    h = hscr[...]
    mu = jnp.mean(h, axis=-1, keepdims=True)
    hc = h - mu
    var = jnp.mean(hc * hc, axis=-1, keepdims=True)
    y = hc * lax.rsqrt(var + EPS) * lng_ref[...] + lnb_ref[...]
    yc_ref[...] = y * jax.nn.sigmoid(y)


def _branch_bc(proj, n_seq, t, st_pool, st_conv, wp_bf, ls, dw, bc, lng, lnb, pos0):
    tm = min(256, t)
    nt = t // tm
    n = n_seq * t
    row = lambda c: pl.BlockSpec((tm, COL), lambda b, i, c=c: (b * nt + i, c))
    full = lambda shape: pl.BlockSpec(shape, lambda b, i: tuple(0 for _ in shape))
    out = pl.BlockSpec((tm, COL), lambda b, i: (b * nt + i, 0))
    return pl.pallas_call(
        functools.partial(_bc_kernel, tm=tm, pos0=pos0),
        grid=(n_seq, nt),
        in_specs=[
            row(CT_UB), row(CT_UC), row(CT_UC + 1),
            pl.BlockSpec((1, POOL_PAD, C_B), lambda b, i: (b, 0, 0)),
            pl.BlockSpec((1, CONV_PAD, C_C), lambda b, i: (b, 0, 0)),
            full((4, POOL_GC, POOL_GC)), full((1, C_B)), full((CONV_PAD, C_C)), full((1, C_C)),
            full((1, C_C)), full((1, C_C)),
        ],
        out_specs=[out, out, out],
        out_shape=[jax.ShapeDtypeStruct((n, COL), F32)] * 3,
        scratch_shapes=[pltpu.VMEM((tm + POOL_PAD, C_B), F32), pltpu.VMEM((tm + CONV_PAD, C_C), F32),
                        pltpu.VMEM((tm, C_C), F32)],
        compiler_params=_cparams("arbitrary", "arbitrary"),
        name="pool_conv",
    )(proj, proj, proj, st_pool, st_conv, wp_bf, ls, dw, bc, lng, lnb)


def _merge_kernel(x_ref, ya_ref, yb_ref, yc_ref, yd_ref, *rest):
    g_refs = rest[:2 * N_BRANCH]
    wb_ref, wo_ref, o_ref = rest[2 * N_BRANCH:]
    merged = None
    for n, y_ref in enumerate((ya_ref, yb_ref, yc_ref, yd_ref)):
        br = jnp.dot(y_ref[...].astype(BF16), wb_ref[n], preferred_element_type=F32)
        g = jnp.concatenate([g_refs[2 * n][...], g_refs[2 * n + 1][...]], axis=-1)
        merged = g * br if merged is None else merged + g * br
    o_ref[...] = x_ref[...] + jnp.dot(merged.astype(BF16), wo_ref[...], preferred_element_type=F32)


def _merge(x, ya, yb, yc, yd, proj, wb_bf, wo_bf):
    n = x.shape[0]
    tm = min(512, n)
    row = lambda w: pl.BlockSpec((tm, w), lambda i: (i, 0))
    gate = lambda c: pl.BlockSpec((tm, COL), lambda i, c=c: (i, CT_GATE + c))
    return pl.pallas_call(
        _merge_kernel,
        grid=(n // tm,),
        in_specs=[row(D_MODEL), row(COL), row(COL), row(COL), row(COL)]
        + [gate(c) for c in range(2 * N_BRANCH)]
        + [pl.BlockSpec((N_BRANCH, BRANCH_W, D_MODEL), lambda i: (0, 0, 0)),
           pl.BlockSpec((D_MODEL, D_MODEL), lambda i: (0, 0))],
        out_specs=row(D_MODEL),
        out_shape=jax.ShapeDtypeStruct((n, D_MODEL), F32),
        compiler_params=_cparams("arbitrary"),
        name="merge",
    )(x, ya, yb, yc, yd, *([proj] * (2 * N_BRANCH)), wb_bf, wo_bf)


def _ffn_kernel(*refs, tm, seq_len, has_state):
    if has_state:
        (x_ref, ng_ref, wg_ref, wv_ref, wc_ref, bc_ref, wd_ref, s1_ref, s2_ref,
         o_ref, gt_ref, h_scr, gext, carry) = refs
    else:
        (x_ref, ng_ref, wg_ref, wv_ref, wc_ref, bc_ref, wd_ref,
         o_ref, gt_ref, h_scr, gext, carry) = refs
    i = pl.program_id(0)
    c = pl.program_id(1)

    @pl.when(c == 0)
    def _():
        x = x_ref[...]
        ms = jnp.mean(x * x, axis=-1, keepdims=True)
        h_scr[...] = (x * lax.rsqrt(ms + EPS) * ng_ref[...]).astype(BF16)
        o_ref[...] = x

    @pl.when(i == 0)
    def _():
        carry[c] = jnp.zeros((FFN_PAD, FFN_CK), F32)

    h = h_scr[...]
    gf = jnp.dot(h, wg_ref[...], preferred_element_type=F32)
    vf = jnp.dot(h, wv_ref[...], preferred_element_type=F32)
    gext[0:FFN_PAD, :] = carry[c]
    gext[FFN_PAD:FFN_PAD + tm, :] = gf
    carry[c] = gext[tm:tm + FFN_PAD, :]
    pos = (i * tm + lax.broadcasted_iota(jnp.int32, (tm, 1), 0)) % seq_len
    sh1 = jnp.where(pos >= 1, gext[FFN_PAD - 1:FFN_PAD - 1 + tm, :], 0.0)
    sh2 = jnp.where(pos >= 2, gext[FFN_PAD - 2:FFN_PAD - 2 + tm, :], 0.0)
    if has_state:
        sh1 = sh1 + s1_ref[...]
        sh2 = sh2 + s2_ref[...]
        gt_ref[...] = gf
    else:
        gt_ref[0] = gext[tm:tm + FFN_PAD, :]
    gc = wc_ref[0:1, :] * sh2 + wc_ref[1:2, :] * sh1 + wc_ref[2:3, :] * gf + bc_ref[...]
    act = jax.nn.gelu(gc) * vf
    o_ref[...] += jnp.dot(act.astype(BF16), wd_ref[...], preferred_element_type=F32)


def _ffn(x, norm_g, wup_bf, wc_pad, bc, wdn_bf, seq_len, s1=None, s2=None):
    n = x.shape[0]
    tm = min(1024, n)
    nck = D_FF // FFN_CK
    has_state = s1 is not None
    row = pl.BlockSpec((tm, D_MODEL), lambda i, c: (i, 0))
    ck = lambda r: pl.BlockSpec((r, FFN_CK), lambda i, c: (0, c))
    in_specs = [row, pl.BlockSpec((1, D_MODEL), lambda i, c: (0, 0)),
                pl.BlockSpec((D_MODEL, FFN_CK), lambda i, c: (0, c)),
                pl.BlockSpec((D_MODEL, FFN_CK), lambda i, c: (0, nck + c)),
                ck(8), ck(1), pl.BlockSpec((FFN_CK, D_MODEL), lambda i, c: (c, 0))]
    args = [x, norm_g, wup_bf, wup_bf, wc_pad, bc, wdn_bf]
    if has_state:
        in_specs += [pl.BlockSpec((tm, FFN_CK), lambda i, c: (i, c))] * 2
        args += [s1, s2]
        gt_spec = pl.BlockSpec((tm, FFN_CK), lambda i, c: (i, c))
        gt_shape = jax.ShapeDtypeStruct((n, D_FF), F32)
    else:
        gt_spec = pl.BlockSpec((1, FFN_PAD, FFN_CK), lambda i, c: (i, 0, c))
        gt_shape = jax.ShapeDtypeStruct((n // tm, FFN_PAD, D_FF), F32)
    return pl.pallas_call(
        functools.partial(_ffn_kernel, tm=tm, seq_len=seq_len, has_state=has_state),
        grid=(n // tm, nck),
        in_specs=in_specs,
        out_specs=[row, gt_spec],
        out_shape=[jax.ShapeDtypeStruct((n, D_MODEL), F32), gt_shape],
        scratch_shapes=[pltpu.VMEM((tm, D_MODEL), BF16), pltpu.VMEM((tm + FFN_PAD, FFN_CK), F32),
                        pltpu.VMEM((nck, FFN_PAD, FFN_CK), F32)],
        compiler_params=_cparams("arbitrary", "arbitrary"),
        name="conv_ffn",
    )(*args)


def _kmean_kernel(k_ref, o_ref):
    nb = o_ref.shape[0]
    k = k_ref[...].reshape(nb, MOBA_BLOCK, COL)
    o_ref[...] = jnp.mean(k, axis=1)


def _kmean(proj, n_blocks):
    nb = 8
    assert n_blocks % nb == 0
    return pl.pallas_call(
        _kmean_kernel,
        grid=(n_blocks // nb,),
        in_specs=[pl.BlockSpec((nb * MOBA_BLOCK, COL), lambda i: (i, CT_KA))],
        out_specs=pl.BlockSpec((nb, COL), lambda i: (i, 0)),
        out_shape=jax.ShapeDtypeStruct((n_blocks, COL), F32),
        compiler_params=_cparams("arbitrary"),
        name="moba_kmean",
    )(proj)


def _top3_bias(gate, n_past):
    nb = gate.shape[0]
    jj = lax.broadcasted_iota(jnp.int32, gate.shape, 0)
    past = jj < n_past
    jj = jj.astype(F32)
    rem = jnp.where(past, gate, NEG)
    sel = jnp.zeros(gate.shape, F32)
    for _ in range(MOBA_TOPK):
        m = jnp.max(rem, axis=0, keepdims=True)
        idx = jnp.min(jnp.where(rem == m, jj, float(nb)), axis=0, keepdims=True)
        pick = jj == idx
        sel = jnp.where(pick, 1.0, sel)
        rem = jnp.where(pick, -jnp.inf, rem)
    return jnp.where(past & (sel > 0.5), 0.0, NEG)


def _gate_kernel(q_ref, km_ref, o_ref, *, nb):
    i = pl.program_id(1)
    gate = lax.dot_general(km_ref[0], q_ref[...], (((1,), (1,)), ((), ())),
                           precision=lax.Precision.HIGHEST, preferred_element_type=F32)
    for h in range(H_A):
        o_ref[0, h] = _top3_bias(gate[h * nb:(h + 1) * nb, :], i)


def _moba_gate(proj, kmbd, n_seq, t):
    nb = t // MOBA_BLOCK
    tq = MOBA_BLOCK
    return pl.pallas_call(
        functools.partial(_gate_kernel, nb=nb),
        grid=(n_seq, nb),
        in_specs=[pl.BlockSpec((tq, COL), lambda b, i: (b * nb + i, CT_QA)),
                  pl.BlockSpec((1, H_A * nb, COL), lambda b, i: (b, 0, 0))],
        out_specs=pl.BlockSpec((1, H_A, nb, tq), lambda b, i: (b, 0, 0, i)),
        out_shape=jax.ShapeDtypeStruct((n_seq, H_A, nb, t), F32),
        compiler_params=_cparams("arbitrary", "arbitrary"),
        name="moba_gate",
    )(proj, kmbd)


def _causal_mask(n):
    r = lax.broadcasted_iota(jnp.int32, (n, n), 0)
    c = lax.broadcasted_iota(jnp.int32, (n, n), 1)
    return r <= c


def _moba_kernel(qT_ref, k_ref, vT_ref, bias_ref, o_ref):
    i = pl.program_id(2)
    blk = MOBA_BLOCK
    qT = qT_ref[0, 0]

    def scores(j):
        kj = k_ref[0, 0, pl.ds(pl.multiple_of(j * blk, blk), blk), :]
        return jnp.dot(kj, qT, preferred_element_type=F32)

    s = jnp.where(_causal_mask(blk), scores(i), NEG)
    m0 = jnp.max(s, axis=0, keepdims=True)
    p = jnp.exp(s - m0)
    l0 = jnp.sum(p, axis=0, keepdims=True)
    acc0 = jnp.dot(vT_ref[0, 0, i], p.astype(BF16), preferred_element_type=F32)

    def body(j, carry):
        m, l, acc = carry
        s = scores(j)
        t = bias_ref[0, 0, pl.ds(j, 1), :]
        m_new = jnp.maximum(m, jnp.max(s, axis=0, keepdims=True) + t)
        p = jnp.exp(s - (m_new - t))
        alpha = jnp.exp(m - m_new)
        l = l * alpha + jnp.sum(p, axis=0, keepdims=True)
        acc = acc * alpha + jnp.dot(vT_ref[0, 0, j], p.astype(BF16), preferred_element_type=F32)
        return m_new, l, acc

    _, l, acc = lax.fori_loop(0, i, body, (m0, l0, acc0))
    o_ref[0] = acc / l


def _moba_prompt(qT, k, vT, bias):
    n_seq, _, _, t = qT.shape
    nb = t // MOBA_BLOCK
    tq = MOBA_BLOCK
    return pl.pallas_call(
        _moba_kernel,
        grid=(n_seq, H_A, nb),
        in_specs=[pl.BlockSpec((1, 1, AUG, tq), lambda b, h, i: (b, h, 0, i)),
                  pl.BlockSpec((1, 1, t, AUG), lambda b, h, i: (b, h, 0, 0)),
                  pl.BlockSpec((1, 1, nb, HEAD_DIM, MOBA_BLOCK), lambda b, h, i: (b, h, 0, 0, 0)),
                  pl.BlockSpec((1, 1, nb, tq), lambda b, h, i: (b, h, 0, i))],
        out_specs=pl.BlockSpec((1, HEAD_DIM, tq), lambda b, h, i: (b, h, i)),
        out_shape=jax.ShapeDtypeStruct((n_seq, H_A * HEAD_DIM, t), F32),
        compiler_params=_cparams("arbitrary", "arbitrary", "arbitrary"),
        name="moba_attn",
    )(qT, k, vT, bias)


def _lambda(lam_ref, lam_init):
    e1 = jnp.exp(jnp.sum(lam_ref[0:1, :] * lam_ref[1:2, :], axis=-1, keepdims=True))
    e2 = jnp.exp(jnp.sum(lam_ref[2:3, :] * lam_ref[3:4, :], axis=-1, keepdims=True))
    return e1 - e2 + lam_init


def _diff_kernel(qT_ref, k_ref, vT_ref, lam_ref, hg_ref, o_ref, *, lam_init):
    i = pl.program_id(2)
    blk = MOBA_BLOCK

    def scores(c, j):
        kj = k_ref[0, 0, c, pl.ds(pl.multiple_of(j * blk, blk), blk), :]
        return jnp.dot(kj, qT_ref[0, 0, c], preferred_element_type=F32)

    mask = _causal_mask(blk)
    init = []
    for c in range(2):
        s = jnp.where(mask, scores(c, i), NEG)
        m = jnp.max(s, axis=0, keepdims=True)
        p = jnp.exp(s - m)
        init += [m, jnp.sum(p, axis=0, keepdims=True),
                 jnp.dot(vT_ref[0, 0, i], p.astype(BF16), preferred_element_type=F32)]

    def body(j, carry):
        out = []
        vj = vT_ref[0, 0, j]
        for c in range(2):
            m, l, acc = carry[3 * c:3 * c + 3]
            s = scores(c, j)
            m_new = jnp.maximum(m, jnp.max(s, axis=0, keepdims=True))
            p = jnp.exp(s - m_new)
            alpha = jnp.exp(m - m_new)
            out += [m_new, l * alpha + jnp.sum(p, axis=0, keepdims=True),
                    acc * alpha + jnp.dot(vj, p.astype(BF16), preferred_element_type=F32)]
        return tuple(out)

    _, l0, a0, _, l1, a1 = lax.fori_loop(0, i, body, tuple(init))
    lam = _lambda(lam_ref, lam_init)
    o = a0 / l0 - lam * (a1 / l1)
    ms = jnp.mean(o * o, axis=0, keepdims=True)
    o_ref[0] = o * lax.rsqrt(ms + EPS) * hg_ref[...] * (1.0 - lam_init)


def _diff_prompt(qT, k, vT, lam4, hg_col, lam_init):
    n_seq, _, _, _, t = qT.shape
    nb = t // MOBA_BLOCK
    tq = MOBA_BLOCK
    return pl.pallas_call(
        functools.partial(_diff_kernel, lam_init=lam_init),
        grid=(n_seq, H_D, nb),
        in_specs=[pl.BlockSpec((1, 1, 2, AUG, tq), lambda b, h, i: (b, h, 0, 0, i)),
                  pl.BlockSpec((1, 1, 2, t, AUG), lambda b, h, i: (b, h, 0, 0, 0)),
                  pl.BlockSpec((1, 1, nb, 2 * HEAD_DIM, MOBA_BLOCK), lambda b, h, i: (b, h, 0, 0, 0)),
                  pl.BlockSpec((4, HEAD_DIM), lambda b, h, i: (0, 0)),
                  pl.BlockSpec((2 * HEAD_DIM, 1), lambda b, h, i: (0, 0))],
        out_specs=pl.BlockSpec((1, 2 * HEAD_DIM, tq), lambda b, h, i: (b, h, i)),
        out_shape=jax.ShapeDtypeStruct((n_seq, H_D * 2 * HEAD_DIM, t), F32),
        compiler_params=_cparams("arbitrary", "arbitrary", "arbitrary"),
        name="diff_attn",
    )(qT, k, vT, lam4, hg_col)


def _decode_kernel(pt_ref, *refs, n_pages, page, t_new, moba, lam_init):
    del pt_ref
    k_pages = refs[:n_pages]
    v_pages = refs[n_pages:2 * n_pages]
    (qbd_ref, kn_ref, vn_ref, bias_ref, dmask_ref, gm_ref, lam_ref, hg_ref,
     o_ref, s_scr, v_scr, new_scr, km_scr) = refs[2 * n_pages:]
    past = n_pages * page
    qbd = qbd_ref[0]
    qbd_bf = qbd.astype(BF16)
    nt = (((1,), (1,)), ((), ()))
    per_blk = MOBA_BLOCK // page

    for p in range(n_pages):
        kp = k_pages[p][0, 0]
        s_scr[:, p * page:(p + 1) * page] = lax.dot_general(qbd_bf, kp.astype(BF16), nt,
                                                            preferred_element_type=F32)
        v_scr[p * page:(p + 1) * page, :] = v_pages[p][0, 0].astype(BF16)
        if moba:
            ksum = jnp.sum(kp, axis=0, keepdims=True)
            j = p // per_blk
            if p % per_blk == 0:
                km_scr[j:j + 1, :] = ksum
            else:
                km_scr[j:j + 1, :] = km_scr[j:j + 1, :] + ksum

    new_scr[...] = jnp.zeros(new_scr.shape, F32)
    new_scr[0:t_new, :] = kn_ref[0]
    s_scr[:, past:past + page] = lax.dot_general(qbd_bf, new_scr[...].astype(BF16), nt,
                                                 preferred_element_type=F32)
    new_scr[0:t_new, :] = vn_ref[0]
    v_scr[past:past + page, :] = new_scr[...].astype(BF16)

    if moba:
        nblk = past // MOBA_BLOCK
        gate = lax.dot_general(qbd, km_scr[...], nt, precision=lax.Precision.HIGHEST,
                               preferred_element_type=F32)
        jj = lax.broadcasted_iota(jnp.int32, gate.shape, 1).astype(F32)
        rem = gate
        sel = jnp.zeros(gate.shape, F32)
        for _ in range(MOBA_TOPK):
            m = jnp.max(rem, axis=1, keepdims=True)
            idx = jnp.min(jnp.where(rem == m, jj, float(nblk)), axis=1, keepdims=True)
            pick = jj == idx
            sel = jnp.where(pick, 1.0, sel)
            rem = jnp.where(pick, -jnp.inf, rem)
        for j in range(nblk):
            lo = j * MOBA_BLOCK
            sj = s_scr[:, lo:lo + MOBA_BLOCK] + bias_ref[:, lo:lo + MOBA_BLOCK]
            s_scr[:, lo:lo + MOBA_BLOCK] = jnp.where(sel[:, j:j + 1] > 0.5, sj, NEG)
        s_scr[:, past:past + page] = s_scr[:, past:past + page] + bias_ref[:, past:past + page]
        s = s_scr[...]
    else:
        s = s_scr[...] + bias_ref[...]

    m = jnp.max(s, axis=1, keepdims=True)
    p = jnp.exp(s - m)
    p = p / jnp.sum(p, axis=1, keepdims=True)
    if moba:
        a = p
        n_out_heads = H_A
    else:
        lam = _lambda(lam_ref, lam_init)
        p4 = p.reshape(H_D, 2, t_new, p.shape[-1])
        a = (p4[:, 0] - lam * p4[:, 1]).reshape(H_D * t_new, p.shape[-1])
        n_out_heads = H_D
    o_full = jnp.dot(a.astype(BF16), v_scr[...], preferred_element_type=F32)
    o = jnp.sum((o_full * dmask_ref[...]).reshape(n_out_heads, t_new, COL), axis=0)
    if not moba:
        ms = jnp.dot((o * o).astype(BF16), gm_ref[...], preferred_element_type=F32)
        o = o * lax.rsqrt(ms + EPS) * hg_ref[...] * (1.0 - lam_init)
    o_ref[0] = o


def _decode_attn(page_table, cache_k, cache_v, layer, qbd, k_new, v_new, bias, dmask, gm, lam4, hg_row,
                 moba, lam_init):
    n_seq, n_pages = page_table.shape
    page = cache_k.shape[2]
    t_new = k_new.shape[1]
    keys = n_pages * page + page
    n_rows = qbd.shape[1]
    n_out_rows = dmask.shape[0]
    pg = lambda p: pl.BlockSpec((1, 1, page, COL), lambda b, pt, p=p: (layer, pt[b, p], 0, 0))
    full = lambda shape: pl.BlockSpec(shape, lambda b, pt: tuple(0 for _ in shape))
    grid_spec = pltpu.PrefetchScalarGridSpec(
        num_scalar_prefetch=1,
        grid=(n_seq,),
        in_specs=[pg(p) for p in range(n_pages)] * 2
        + [pl.BlockSpec((1, n_rows, COL), lambda b, pt: (b, 0, 0)),
           pl.BlockSpec((1, t_new, COL), lambda b, pt: (b, 0, 0)),
           pl.BlockSpec((1, t_new, COL), lambda b, pt: (b, 0, 0)),
           full((n_rows, keys)), full((n_out_rows, COL)), full((COL, COL)), full((4, HEAD_DIM)), full((1, COL))],
        out_specs=pl.BlockSpec((1, t_new, COL), lambda b, pt: (b, 0, 0)),
        scratch_shapes=[pltpu.VMEM((n_rows, keys), F32), pltpu.VMEM((keys, COL), BF16),
                        pltpu.VMEM((page, COL), F32), pltpu.VMEM((n_pages * page // MOBA_BLOCK, COL), F32)],
    )
    return pl.pallas_call(
        functools.partial(_decode_kernel, n_pages=n_pages, page=page, t_new=t_new, moba=moba, lam_init=lam_init),
        grid_spec=grid_spec,
        out_shape=jax.ShapeDtypeStruct((n_seq, t_new, COL), F32),
        compiler_params=_cparams("arbitrary"),
        name="decode_moba" if moba else "decode_diff",
    )(page_table, *([cache_k] * n_pages), *([cache_v] * n_pages), qbd, k_new, v_new, bias, dmask, gm, lam4, hg_row)


def _block_diag_mean(group):
    g = np.arange(COL) // group
    return jnp.asarray((g[:, None] == g[None, :]).astype(np.float32) / group, BF16)


def _aug_q(q, t):
    lead = q.shape[:-2]
    pos = jnp.arange(t, dtype=jnp.int32)
    ones = jnp.ones((t,), F32)
    own = -(MOBA_BLOCK * (pos // MOBA_BLOCK)).astype(F32)
    extra = jnp.stack([ones, ones, own], axis=0)
    extra = jnp.broadcast_to(extra, lead + (3, t))
    pad = jnp.zeros(lead + (AUG - HEAD_DIM - 3, t), F32)
    return jnp.concatenate([jnp.swapaxes(q, -1, -2), extra, pad], axis=-2).astype(BF16)


def _aug_k(k, slopes, t):
    pos = jnp.arange(t, dtype=jnp.int32)
    base = jnp.stack([(pos % MOBA_BLOCK).astype(F32), (MOBA_BLOCK * (pos // MOBA_BLOCK)).astype(F32),
                      jnp.ones((t,), F32)], axis=-1)
    sl = jnp.asarray(slopes, F32).reshape((1, len(slopes)) + (1,) * (k.ndim - 2))
    extra = sl * base.reshape((1,) * (k.ndim - 2) + (t, 3))
    extra = jnp.broadcast_to(extra, k.shape[:-1] + (3,))
    pad = jnp.zeros(k.shape[:-1] + (AUG - HEAD_DIM - 3,), F32)
    return jnp.concatenate([k, extra, pad], axis=-1).astype(BF16)


def _decode_consts(n_heads_virtual, heads_per_slope, slopes, t_new, past, page):
    keys = past + page
    rows = n_heads_virtual * t_new
    hv = np.arange(rows) // t_new
    q = np.arange(rows) % t_new
    slope = np.asarray(slopes, np.float64)[hv // heads_per_slope]
    kpos = np.arange(keys)
    qpos = past + q
    bias = -slope[:, None] * (qpos[:, None] - kpos[None, :])
    valid = kpos[None, :] <= qpos[:, None]
    return jnp.asarray(np.where(valid, bias, NEG).astype(np.float32))


def _diag_mask(n_heads, t_new, width):
    rows = np.arange(n_heads * t_new) // t_new
    cols = np.arange(COL) // width
    return jnp.asarray((rows[:, None] == cols[None, :]).astype(np.float32))


def _q_block_diag(q, n_virtual):
    n_seq, t_new, _ = q.shape
    head_of_col = jnp.arange(COL) // HEAD_DIM
    m = (jnp.arange(n_virtual)[:, None] == head_of_col[None, :]).astype(F32)
    return (q[:, None, :, :] * m[None, :, None, :]).reshape(n_seq, n_virtual * t_new, COL)


def _layer(l, lam_init, xp, xs, n_p, t_p, n_s, t_s, caches, states, page_table, w):
    scale = HEAD_DIM ** -0.5
    slopes_a = _alibi_slopes(H_A)
    slopes_d = _alibi_slopes(H_D)
    gm64 = _block_diag_mean(HEAD_DIM)
    gm128 = _block_diag_mean(2 * HEAD_DIM)

    gain_row = jnp.ones((1, D_IN), F32)
    for ct, g in ((CT_QA, w['qnorm_a']), (CT_KA, w['knorm_a']), (CT_QD, w['qnorm_d']), (CT_KD, w['knorm_d'])):
        gain_row = gain_row.at[0, ct * COL:(ct + 1) * COL].set(jnp.tile(g[l], COL // HEAD_DIM))
    bias_row = jnp.zeros((1, D_IN), F32).at[0, CT_GATE * COL:].set(w['b_gate'][l])
    w_in_bf = w['w_in'][l].astype(BF16)
    wp_bf = w['w_pool'][l].astype(BF16)
    wb_bf = w['w_branch'][l].astype(BF16)
    wo_bf = w['w_out'][l].astype(BF16)
    wup_bf = w['w_up'][l].astype(BF16)
    wdn_bf = w['w_down'][l].astype(BF16)
    dw_pad = jnp.zeros((CONV_PAD, C_C), F32).at[:CONV_W].set(w['dw_conv'][l])
    wc_pad = jnp.zeros((8, D_FF), F32).at[:FFN_CONV_W].set(w['w_ffn_conv'][l])
    row = lambda v: v[l].reshape(1, -1)
    lam4 = jnp.stack([w['lam_q1'][l], w['lam_k1'][l], w['lam_q2'][l], w['lam_k2'][l]], axis=0)
    hg = w['hnorm_d'][l]

    def dense_front(x):
        return _in_proj(x, row(w['norm_mix']), w_in_bf, gain_row, bias_row, gm64)

    def col(proj, ct, n=1):
        return proj[:, ct * COL:(ct + n) * COL]

    def tail(x, proj, ya, yd, n_seq, t, st_pool, st_conv, pos0, seq_len, s1=None, s2=None):
        yb, yc, u = _branch_bc(proj, n_seq, t, st_pool, st_conv, wp_bf, row(w['ls_pool']), dw_pad,
                               row(w['b_conv']), row(w['ln_conv_g']), row(w['ln_conv_b']), pos0)
        x = _merge(x, ya, yb, yc, yd, proj, wb_bf, wo_bf)
        x, gt = _ffn(x, row(w['norm_ffn']), wup_bf, wc_pad, row(w['b_ffn_conv']), wdn_bf, seq_len, s1, s2)
        return x, u, gt

    proj = dense_front(xp)
    nb = t_p // MOBA_BLOCK
    heads = lambda a, h: a.reshape(n_p, t_p, h, -1)
    qa = heads(col(proj, CT_QA), H_A).transpose(0, 2, 1, 3) * scale
    ka = heads(col(proj, CT_KA), H_A).transpose(0, 2, 1, 3)
    va = heads(col(proj, CT_VA), H_A).transpose(0, 2, 3, 1)
    vaT = va.reshape(n_p, H_A, HEAD_DIM, nb, MOBA_BLOCK).transpose(0, 1, 3, 2, 4).astype(BF16)
    kmean = _kmean(proj, n_p * nb).reshape(n_p, nb, H_A, HEAD_DIM)
    eye = jnp.eye(H_A, dtype=F32)
    kmbd = jnp.einsum('bjhd,hg->bhjgd', kmean, eye).reshape(n_p, H_A * nb, COL)
    bias = _moba_gate(proj, kmbd, n_p, t_p)
    yaT = _moba_prompt(_aug_q(qa, t_p), _aug_k(ka, slopes_a, t_p), vaT, bias)
    ya = yaT.transpose(0, 2, 1).reshape(n_p * t_p, COL)
    qd = col(proj, CT_QD).reshape(n_p, t_p, H_D, 2, HEAD_DIM).transpose(0, 2, 3, 1, 4) * scale
    kd = col(proj, CT_KD).reshape(n_p, t_p, H_D, 2, HEAD_DIM).transpose(0, 2, 3, 1, 4)
    vd = heads(col(proj, CT_VD), H_D).transpose(0, 2, 3, 1)
    vdT = vd.reshape(n_p, H_D, 2 * HEAD_DIM, nb, MOBA_BLOCK).transpose(0, 1, 3, 2, 4).astype(BF16)
    ydT = _diff_prompt(_aug_q(qd, t_p), _aug_k(kd, slopes_d, t_p), vdT, lam4, hg.reshape(-1, 1), lam_init)
    yd = ydT.transpose(0, 2, 1).reshape(n_p * t_p, COL)
    zp = jnp.zeros((n_p, POOL_PAD, C_B), F32)
    zc = jnp.zeros((n_p, CONV_PAD, C_C), F32)
    xp_new, u_p, gt_p = tail(xp, proj, ya, yd, n_p, t_p, zp, zc, 0, t_p)
    tiles_per_seq = t_p // min(1024, n_p * t_p)
    p_state = (
        col(proj, CT_KA).reshape(n_p, t_p, H_A, HEAD_DIM),
        col(proj, CT_VA).reshape(n_p, t_p, H_A, HEAD_DIM),
        col(proj, CT_KD).reshape(n_p, t_p, H_D, 2, HEAD_DIM),
        col(proj, CT_VD).reshape(n_p, t_p, H_D, 2 * HEAD_DIM),
        col(proj, CT_UB).reshape(n_p, t_p, C_B)[:, -POOL_STATE:],
        u_p.reshape(n_p, t_p, C_C)[:, -(CONV_W - 1):],
        gt_p.reshape(n_p, tiles_per_seq, FFN_PAD, D_FF)[:, -1, -(FFN_CONV_W - 1):],
    )

    cache_a_k, cache_a_v, cache_d_k, cache_d_v = caches
    st_pool, st_conv, st_ffn = states
    projs = dense_front(xs)
    n_pages = page_table.shape[1]
    page = cache_a_k.shape[2]
    past = n_pages * page
    seq3 = lambda a: a.reshape(n_s, t_s, COL)
    bias_a = _decode_consts(H_A, 1, slopes_a, t_s, past, page)
    bias_d = _decode_consts(2 * H_D, 2, slopes_d, t_s, past, page)
    hg_row = jnp.tile(hg, H_D).reshape(1, COL)
    ya_s = _decode_attn(page_table, cache_a_k, cache_a_v, l, _q_block_diag(seq3(col(projs, CT_QA)) * scale, H_A),
                        seq3(col(projs, CT_KA)), seq3(col(projs, CT_VA)), bias_a, _diag_mask(H_A, t_s, HEAD_DIM),
                        gm128, lam4, hg_row, True, lam_init)
    yd_s = _decode_attn(page_table, cache_d_k, cache_d_v, l, _q_block_diag(seq3(col(projs, CT_QD)) * scale, 2 * H_D),
                        seq3(col(projs, CT_KD)), seq3(col(projs, CT_VD)), bias_d, _diag_mask(H_D, t_s, 2 * HEAD_DIM),
                        gm128, lam4, hg_row, False, lam_init)
    sp = jnp.pad(st_pool[l], ((0, 0), (POOL_PAD - POOL_STATE, 0), (0, 0)))
    sc = jnp.pad(st_conv[l], ((0, 0), (CONV_PAD - (CONV_W - 1), 0), (0, 0)))
    sf = st_ffn[l]
    zrow = jnp.zeros((n_s, 1, D_FF), F32)
    s1 = jnp.concatenate([sf[:, 1:2]] + [zrow] * (t_s - 1), axis=1).reshape(n_s * t_s, D_FF)
    s2 = jnp.concatenate([sf[:, 0:1], sf[:, 1:2]] + [zrow] * (t_s - 2), axis=1).reshape(n_s * t_s, D_FF)
    xs_new, u_s, gt_s = tail(xs, projs, ya_s.reshape(n_s * t_s, COL), yd_s.reshape(n_s * t_s, COL),
                             n_s, t_s, sp, sc, past, t_s, s1, s2)
    ext = lambda st, new, keep: jnp.concatenate([st, new], axis=1)[:, -keep:]
    s_state = (
        col(projs, CT_KA).reshape(n_s, t_s, H_A, HEAD_DIM),
        col(projs, CT_VA).reshape(n_s, t_s, H_A, HEAD_DIM),
        col(projs, CT_KD).reshape(n_s, t_s, H_D, 2, HEAD_DIM),
        col(projs, CT_VD).reshape(n_s, t_s, H_D, 2 * HEAD_DIM),
        ext(st_pool[l], col(projs, CT_UB).reshape(n_s, t_s, C_B), POOL_STATE),
        ext(st_conv[l], u_s.reshape(n_s, t_s, C_C), CONV_W - 1),
        ext(sf, gt_s.reshape(n_s, t_s, D_FF), FFN_CONV_W - 1),
    )
    return xp_new, xs_new, p_state, s_state


def kernel(x_prompt, x_sample, cache_a_k, cache_a_v, cache_d_k, cache_d_v, state_pool, state_conv, state_ffn, page_table, norm_mix, w_in, b_gate, qnorm_a, knorm_a, w_pool, ls_pool, dw_conv, b_conv, ln_conv_g, ln_conv_b, qnorm_d, knorm_d, lam_q1, lam_k1, lam_q2, lam_k2, hnorm_d, w_branch, w_out, norm_ffn, w_up, w_ffn_conv, b_ffn_conv, w_down):
    w = dict(norm_mix=norm_mix, w_in=w_in, b_gate=b_gate, qnorm_a=qnorm_a, knorm_a=knorm_a, w_pool=w_pool,
             ls_pool=ls_pool, dw_conv=dw_conv, b_conv=b_conv, ln_conv_g=ln_conv_g, ln_conv_b=ln_conv_b,
             qnorm_d=qnorm_d, knorm_d=knorm_d, lam_q1=lam_q1, lam_k1=lam_k1, lam_q2=lam_q2, lam_k2=lam_k2,
             hnorm_d=hnorm_d, w_branch=w_branch, w_out=w_out, norm_ffn=norm_ffn, w_up=w_up,
             w_ffn_conv=w_ffn_conv, b_ffn_conv=b_ffn_conv, w_down=w_down)
    depth = w_in.shape[0]
    n_p, t_p, _ = x_prompt.shape
    n_s, t_s, _ = x_sample.shape
    n_pool, page = cache_a_k.shape[1], cache_a_k.shape[2]
    assert t_p % MOBA_BLOCK == 0 and (page_table.shape[1] * page) % MOBA_BLOCK == 0 and MOBA_BLOCK % page == 0
    caches = (cache_a_k.reshape(depth, n_pool, page, COL), cache_a_v.reshape(depth, n_pool, page, COL),
              cache_d_k.reshape(depth, n_pool, page, COL), cache_d_v.reshape(depth, n_pool, page, COL))
    states = (state_pool, state_conv, state_ffn)
    xp = x_prompt.reshape(n_p * t_p, D_MODEL)
    xs = x_sample.reshape(n_s * t_s, D_MODEL)
    p_new, s_new = [], []
    for l in range(depth):
        lam_init = 0.8 - 0.6 * math.exp(-0.3 * l)
        xp, xs, p_state, s_state = _layer(l, lam_init, xp, xs, n_p, t_p, n_s, t_s, caches, states, page_table, w)
        p_new.append(p_state)
        s_new.append(s_state)
    p_out = [jnp.stack([p_new[l][i] for l in range(depth)], axis=0) for i in range(7)]
    s_out = [jnp.stack([s_new[l][i] for l in range(depth)], axis=0) for i in range(7)]
    return (xp.reshape(n_p, t_p, D_MODEL), xs.reshape(n_s, t_s, D_MODEL), *p_out, *s_out)
```
